```python
import math
import jax, jax.numpy as jnp
from jax import lax
import numpy as np

D_MODEL = 1024
BATCH = 4
SEQ = 8192
DEPTH = 2

CONV_DIM = D_MODEL
CONV_KERNEL = 31
SSM_EXPAND = 2
SSM_DIM = SSM_EXPAND * D_MODEL
SSM_HEAD_DIM = 64
SSM_HEADS = SSM_DIM // SSM_HEAD_DIM
SSM_GROUPS = 4
SSM_STATE = 128
SSM_CONV = 4
SSM_CHUNK = 128
SSM_BC = 2 * SSM_GROUPS * SSM_STATE
FFN_DIM = 2816
FFN_CONV = 3
DN_ALPHA = (2 * DEPTH) ** 0.25
DN_BETA = (8 * DEPTH) ** -0.25
LN_EPS = 1e-5
RMS_EPS = 1e-5
IN_SIZES = (2 * CONV_DIM, SSM_DIM, SSM_DIM + SSM_BC, SSM_HEADS, D_MODEL, D_MODEL)
IN_DIM = sum(IN_SIZES)

kernel_name = "hybrid_conformer_ssd_deepnorm"


def layer_norm(x, g, b):
    xf = x.astype(jnp.float32)
    mu = jnp.mean(xf, axis=-1, keepdims=True)
    var = jnp.mean(jnp.square(xf - mu), axis=-1, keepdims=True)
    return ((xf - mu) * lax.rsqrt(var + LN_EPS) * g.astype(jnp.float32)
            + b.astype(jnp.float32)).astype(x.dtype)


def causal_dwconv(x, w, b):
    k = w.shape[0]
    y = lax.conv_general_dilated(
        x, w[:, None, :].astype(x.dtype), window_strides=(1,),
        padding=[(k - 1, 0)], dimension_numbers=("NWC", "WIO", "NWC"),
        feature_group_count=x.shape[-1])
    return y + b.astype(x.dtype)


def split_in(u):
    idx = [int(v) for v in np.cumsum(IN_SIZES)[:-1]]
    return jnp.split(u, idx, axis=-1)


def conformer_branch(u_glu, dw_w, dw_b, ln_g, ln_b, w_out):
    a, g = jnp.split(u_glu, 2, axis=-1)
    v = a * jax.nn.sigmoid(g)
    v = causal_dwconv(v, dw_w, dw_b)
    v = jax.nn.silu(layer_norm(v, ln_g, ln_b))
    return v @ w_out


def ssd_chunked(x, dt, a_head, bm, cm):
    b, t, h, p = x.shape
    g, n = bm.shape[2], bm.shape[3]
    r = h // g
    L = SSM_CHUNK
    c = t // L
    xs = (x * dt[..., None]).reshape(b, c, L, g, r, p)
    a = (dt * a_head).reshape(b, c, L, g, r).transpose(0, 1, 3, 4, 2)
    bc = bm.reshape(b, c, L, g, n)
    cc = cm.reshape(b, c, L, g, n)
    a_cs = jnp.cumsum(a, axis=-1)
    seg = a_cs[..., :, None] - a_cs[..., None, :]
    causal = jnp.tril(jnp.ones((L, L), dtype=bool))
    decay = jnp.where(causal, jnp.exp(jnp.where(causal, seg, 0.0)), 0.0)
    cb = jnp.einsum("bclgn,bcsgn->bcgls", cc, bc)
    y_diag = jnp.einsum("bcgrls,bcsgrp->bclgrp", cb[:, :, :, None] * decay, xs)
    decay_states = jnp.exp(a_cs[..., -1:] - a_cs)
    states = jnp.einsum("bclgn,bcgrl,bclgrp->bcgrpn", bc, decay_states, xs)
    chunk_decay = jnp.exp(a_cs[..., -1])

    def step(carry, inp):
        s_c, d_c = inp
        return d_c[..., None, None] * carry + s_c, carry

    init = jnp.zeros((b, g, r, p, n), states.dtype)
    _, prev = lax.scan(step, init, (jnp.moveaxis(states, 1, 0),
                                    jnp.moveaxis(chunk_decay, 1, 0)))
    prev = jnp.moveaxis(prev, 0, 1)
    y_off = jnp.einsum("bclgn,bcgrpn,bcgrl->bclgrp", cc, prev, jnp.exp(a_cs))
    return (y_diag + y_off).reshape(b, t, h, p)


def ssd_branch(z, xbc, dt_raw, conv_w, conv_b, dt_bias, a_log, d_skip, norm_w, w_out):
    bsz, t, _ = z.shape
    xbc = jax.nn.silu(causal_dwconv(xbc, conv_w, conv_b))
    gn = SSM_GROUPS * SSM_STATE
    xs = xbc[..., :SSM_DIM]
    bm = xbc[..., SSM_DIM:SSM_DIM + gn].reshape(bsz, t, SSM_GROUPS, SSM_STATE)
    cm = xbc[..., SSM_DIM + gn:].reshape(bsz, t, SSM_GROUPS, SSM_STATE)
    dt = jax.nn.softplus(dt_raw.astype(jnp.float32) + dt_bias.astype(jnp.float32))
    a_head = -jnp.exp(a_log.astype(jnp.float32))
    xh = xs.reshape(bsz, t, SSM_HEADS, SSM_HEAD_DIM).astype(jnp.float32)
    y = ssd_chunked(xh, dt, a_head, bm.astype(jnp.float32), cm.astype(jnp.float32))
    y = y + xh * d_skip.astype(jnp.float32)[:, None]
    yg = (y.reshape(bsz, t, SSM_DIM) * jax.nn.silu(z.astype(jnp.float32)))
    yg = yg.reshape(bsz, t, SSM_GROUPS, SSM_DIM // SSM_GROUPS)
    yg = yg * lax.rsqrt(jnp.mean(jnp.square(yg), axis=-1, keepdims=True) + RMS_EPS)
    yn = (yg.reshape(bsz, t, SSM_DIM) * norm_w.astype(jnp.float32)).astype(z.dtype)
    return yn @ w_out


def conv_ffn(h, w_up, dw_w, dw_b, w_down):
    u = causal_dwconv(h @ w_up, dw_w, dw_b)
    gate, val = jnp.split(u, 2, axis=-1)
    return (jax.nn.silu(gate) * val) @ w_down


def setup_inputs(seed: int = 0) -> dict:
    key = jax.random.key(seed)
    ks = jax.random.split(key, 32)
    f32 = jnp.float32
    nrm = lambda k, shape, s: (jax.random.normal(k, shape, f32) * s).astype(f32)
    gain = lambda k, shape: 1.0 + 0.05 * jax.random.normal(k, shape, f32)
    small = lambda k, shape: 0.02 * jax.random.normal(k, shape, f32)
    dt0 = jnp.exp(jax.random.uniform(ks[12], (DEPTH, SSM_HEADS), f32)
                  * (math.log(0.1) - math.log(0.001)) + math.log(0.001))
    return {
        "x": jax.random.normal(ks[0], (BATCH, SEQ, D_MODEL), f32),
        "ln_in_g": gain(ks[1], (D_MODEL,)),
        "ln_in_b": small(ks[2], (D_MODEL,)),
        "w_in": nrm(ks[3], (DEPTH, D_MODEL, IN_DIM), D_MODEL ** -0.5),
        "conv_dw_w": nrm(ks[4], (DEPTH, CONV_KERNEL, CONV_DIM), CONV_KERNEL ** -0.5),
        "conv_dw_b": small(ks[5], (DEPTH, CONV_DIM)),
        "conv_ln_g": gain(ks[6], (DEPTH, CONV_DIM)),
        "conv_ln_b": small(ks[7], (DEPTH, CONV_DIM)),
        "w_conv_out": nrm(ks[8], (DEPTH, CONV_DIM, D_MODEL), DN_BETA * CONV_DIM ** -0.5),
        "ssm_conv_w": nrm(ks[9], (DEPTH, SSM_CONV, SSM_DIM + SSM_BC), SSM_CONV ** -0.5),
        "ssm_conv_b": small(ks[10], (DEPTH, SSM_DIM + SSM_BC)),
        "ssm_dt_bias": dt0 + jnp.log(-jnp.expm1(-dt0)),
        "ssm_a_log": jnp.log(jax.random.uniform(ks[13], (DEPTH, SSM_HEADS), f32, 1.0, 16.0)),
        "ssm_d": gain(ks[14], (DEPTH, SSM_HEADS)),
        "ssm_norm_w": gain(ks[15], (DEPTH, SSM_DIM)),
        "w_ssm_out": nrm(ks[16], (DEPTH, SSM_DIM, D_MODEL), DN_BETA * SSM_DIM ** -0.5),
        "w_o": nrm(ks[17], (DEPTH, D_MODEL, D_MODEL), DN_BETA * D_MODEL ** -0.5),
        "ln1_g": gain(ks[18], (DEPTH, D_MODEL)),
        "ln1_b": small(ks[19], (DEPTH, D_MODEL)),
        "w_ffn_up": nrm(ks[20], (DEPTH, D_MODEL, 2 * FFN_DIM), DN_BETA * D_MODEL ** -0.5),
        "ffn_dw_w": nrm(ks[21], (DEPTH, FFN_CONV, 2 * FFN_DIM), FFN_CONV ** -0.5),
        "ffn_dw_b": small(ks[22], (DEPTH, 2 * FFN_DIM)),
        "w_ffn_down": nrm(ks[23], (DEPTH, FFN_DIM, D_MODEL), DN_BETA * FFN_DIM ** -0.5),
        "ln2_g": gain(ks[24], (DEPTH, D_MODEL)),
        "ln2_b": small(ks[25], (DEPTH, D_MODEL)),
    }


def reference(x, ln_in_g, ln_in_b, w_in, conv_dw_w, conv_dw_b, conv_ln_g, conv_ln_b,
              w_conv_out, ssm_conv_w, ssm_conv_b, ssm_dt_bias, ssm_a_log, ssm_d,
              ssm_norm_w, w_ssm_out, w_o, ln1_g, ln1_b, w_ffn_up, ffn_dw_w, ffn_dw_b,
              w_ffn_down, ln2_g, ln2_b):
    h = layer_norm(x, ln_in_g, ln_in_b)
    for l in range(DEPTH):
        u = h @ w_in[l]
        u_glu, z, xbc, dt_raw, gate_a, gate_b = split_in(u)
        y_a = conformer_branch(u_glu, conv_dw_w[l], conv_dw_b[l], conv_ln_g[l],
                               conv_ln_b[l], w_conv_out[l])
        y_b = ssd_branch(z, xbc, dt_raw, ssm_conv_w[l], ssm_conv_b[l], ssm_dt_bias[l],
                         ssm_a_log[l], ssm_d[l], ssm_norm_w[l], w_ssm_out[l])
        mix = (jax.nn.sigmoid(gate_a) * y_a + jax.nn.sigmoid(gate_b) * y_b) @ w_o[l]
        h = layer_norm(DN_ALPHA * h + mix, ln1_g[l], ln1_b[l])
        ffn = conv_ffn(h, w_ffn_up[l], ffn_dw_w[l], ffn_dw_b[l], w_ffn_down[l])
        h = layer_norm(DN_ALPHA * h + ffn, ln2_g[l], ln2_b[l])
    return h
```

```python
import functools

import jax
import jax.numpy as jnp
from jax import lax
from jax.experimental import pallas as pl
from jax.experimental.pallas import tpu as pltpu

F32 = jnp.float32
BF16 = jnp.bfloat16

D_MODEL = 1024
DEPTH = 2
CONV_DIM = D_MODEL
CONV_KERNEL = 31
SSM_DIM = 2 * D_MODEL
HEAD_DIM = 64
HEADS = SSM_DIM // HEAD_DIM
GROUPS = 4
HEADS_PER_GROUP = HEADS // GROUPS
STATE = 128
SSM_CONV = 4
CHUNK = 128
GROUP_LANES = SSM_DIM // GROUPS
SSM_BC = 2 * GROUPS * STATE
XBC_DIM = SSM_DIM + SSM_BC
FFN_DIM = 2816
FFN_CONV = 3
DN_ALPHA = (2 * DEPTH) ** 0.25
LN_EPS = 1e-5
RMS_EPS = 1e-5

LANES = 128
SUBLANES = 8
VMEM_LIMIT_BYTES = 56 * 1024 * 1024

TILE_CONFORMER = 512
TILE_SSD = 256
TILE_FFN = 512


def _dot(a, b):
    return jnp.dot(a, b, preferred_element_type=F32)


def _layer_norm(x, g, b):
    mu = jnp.mean(x, axis=-1, keepdims=True)
    xc = x - mu
    var = jnp.mean(xc * xc, axis=-1, keepdims=True)
    return xc * lax.rsqrt(var + LN_EPS) * g + b


def _silu(x):
    return x * jax.nn.sigmoid(x)


def _resident(shape):
    zeros = (0,) * len(shape)
    return pl.BlockSpec(shape, lambda b, s: zeros, pipeline_mode=pl.Buffered(1))


def _token_block(tm, width):
    return pl.BlockSpec((None, tm, width), lambda b, s: (b, s, 0))


def _seq_params():
    return pltpu.CompilerParams(
        dimension_semantics=("arbitrary", "arbitrary"),
        vmem_limit_bytes=VMEM_LIMIT_BYTES)


def _load_halo(buf, halo, tm):
    s = pl.program_id(1)

    @pl.when(s == 0)
    def _():
        buf[0:halo, :] = jnp.zeros((halo, buf.shape[1]), buf.dtype)

    @pl.when(s > 0)
    def _():
        buf[0:halo, :] = buf[tm:tm + halo, :]


CONV_ROWS = 64
CONV_COLS = 256


def _dwconv_block(buf, w8_ref, b_ref, r0, halo, bc0, wc0):
    taps = w8_ref.shape[0]
    acc = None
    for r in range(min(SUBLANES, taps)):
        pad = SUBLANES if r else 0
        rows = CONV_ROWS + pad
        z = None
        for q in range((taps - 1 - r) // SUBLANES + 1):
            k = taps - 1 - (SUBLANES * q + r)
            start = pl.multiple_of(r0 + (halo - SUBLANES * q - pad), SUBLANES)
            xw = buf[pl.ds(start, rows), bc0:bc0 + CONV_COLS]
            xw = xw.reshape(rows // SUBLANES, SUBLANES, CONV_COLS)
            term = (w8_ref[k, :, wc0:wc0 + CONV_COLS][None] * xw).reshape(rows, CONV_COLS)
            z = term if z is None else z + term
        zs = z if r == 0 else z[SUBLANES - r:SUBLANES - r + CONV_ROWS]
        acc = zs if acc is None else acc + zs
    return acc + b_ref[:, wc0:wc0 + CONV_COLS]


def _causal_dwconv(buf, w8_ref, b_ref, out_ref, halo, tm, post=None):
    for c0 in range(0, buf.shape[1], CONV_COLS):
        def body(i, carry, c0=c0):
            r0 = pl.multiple_of(i * CONV_ROWS, CONV_ROWS)
            blk = _dwconv_block(buf, w8_ref, b_ref, r0, halo, c0, c0)
            out_ref[pl.ds(r0, CONV_ROWS), c0:c0 + CONV_COLS] = blk if post is None else post(blk)
            return carry

        lax.fori_loop(0, tm // CONV_ROWS, body, 0)


def _ln_kernel(x_ref, g_ref, b_ref, o_ref):
    o_ref[...] = _layer_norm(x_ref[...], g_ref[...], b_ref[...])


def _input_layer_norm(x, g, b, tm):
    bsz, t, d = x.shape
    return pl.pallas_call(
        _ln_kernel,
        grid=(bsz, t // tm),
        in_specs=[_token_block(tm, d), _resident((1, d)), _resident((1, d))],
        out_specs=_token_block(tm, d),
        out_shape=jax.ShapeDtypeStruct(x.shape, F32),
        compiler_params=_seq_params(),
        name="input_ln",
    )(x, g.reshape(1, d), b.reshape(1, d))


CONF_HALO = 32


def _conformer_kernel(h_ref, wglu_ref, wga_ref, dww_ref, dwb_ref, lng_ref, lnb_ref, wout_ref,
                      o_ref, vbuf, cbuf):
    tm = h_ref.shape[0]
    hb = h_ref[...].astype(BF16)
    a = _dot(hb, wglu_ref[:, :CONV_DIM])
    g = _dot(hb, wglu_ref[:, CONV_DIM:])
    _load_halo(vbuf, CONF_HALO, tm)
    vbuf[CONF_HALO:CONF_HALO + tm, :] = a * jax.nn.sigmoid(g)
    _causal_dwconv(vbuf, dww_ref, dwb_ref, cbuf, CONF_HALO, tm)
    v = _silu(_layer_norm(cbuf[...], lng_ref[...], lnb_ref[...]))
    y = _dot(v.astype(BF16), wout_ref[...])
    gate = jax.nn.sigmoid(_dot(hb, wga_ref[...]))
    o_ref[...] = gate * y


def _conformer(h, wglu, wga, dww, dwb, lng, lnb, wout, tm):
    bsz, t, d = h.shape
    return pl.pallas_call(
        _conformer_kernel,
        grid=(bsz, t // tm),
        in_specs=[
            _token_block(tm, d),
            _resident(wglu.shape), _resident(wga.shape), _resident(dww.shape),
            _resident(dwb.shape), _resident(lng.shape), _resident(lnb.shape),
            _resident(wout.shape),
        ],
        out_specs=_token_block(tm, d),
        out_shape=jax.ShapeDtypeStruct((bsz, t, d), F32),
        scratch_shapes=[
            pltpu.VMEM((CONF_HALO + tm, CONV_DIM), F32),
            pltpu.VMEM((tm, CONV_DIM), F32),
        ],
        compiler_params=_seq_params(),
        name="conformer",
    )(h, wglu, wga, dww, dwb, lng, lnb, wout)


SSD_HALO = 8
DT_REPLICAS = LANES // HEADS


def _split3(x):
    hi = x.astype(BF16)
    r1 = x - hi.astype(F32)
    mid = r1.astype(BF16)
    lo = (r1 - mid.astype(F32)).astype(BF16)
    return hi, mid, lo


def _ssd_chunk(c, xbc_ref, dt_ref, y_ref, state_ref, a_head, tril, causal, expand, lane_lt_half):
    L = CHUNK
    r0 = pl.multiple_of(c * L, L)
    dt = dt_ref[pl.ds(r0, L), :]
    a = dt * a_head
    hi, mid, lo = _split3(a)
    p = _dot(tril, jnp.concatenate([hi, mid, lo], axis=1))
    a_cs = p[:, 0:LANES] + p[:, LANES:2 * LANES] + p[:, 2 * LANES:3 * LANES]
    a_cs_t = a_cs.T
    a_last = a_cs[L - 1:L, :]
    decay_states = jnp.exp(a_last - a_cs)
    exp_a = jnp.exp(a_cs)

    lane = lax.broadcasted_iota(jnp.int32, (3 * L, LANES), 1)
    s_hi, s_mid, s_lo = _split3(jnp.concatenate([dt, decay_states, exp_a], axis=0))
    pieces = jnp.where(lane < HEADS, s_hi, jnp.where(lane < 2 * HEADS, s_mid, s_lo))
    ex = _dot(pieces, expand)
    dt_x = ex[0:L]
    ds_x = ex[L:2 * L]
    ea_x = ex[2 * L:3 * L]
    cd_x = ea_x[L - 1:L, :]

    for g in range(GROUPS):
        x0 = g * GROUP_LANES
        xs = xbc_ref[pl.ds(r0, L), x0:x0 + GROUP_LANES]
        bm = xbc_ref[pl.ds(r0, L), SSM_DIM + g * STATE:SSM_DIM + (g + 1) * STATE]
        cm = xbc_ref[pl.ds(r0, L),
                     SSM_DIM + GROUPS * STATE + g * STATE:SSM_DIM + GROUPS * STATE + (g + 1) * STATE]
        bm_b = bm.astype(BF16)
        cm_b = cm.astype(BF16)
        xdt = xs * dt_x[:, x0:x0 + GROUP_LANES]
        cb = lax.dot_general(cm_b, bm_b, (((1,), (1,)), ((), ())), preferred_element_type=F32)

        y_pairs = []
        for pr in range(HEADS_PER_GROUP // 2):
            ms = []
            for hh in range(2):
                h = g * HEADS_PER_GROUP + 2 * pr + hh
                seg = a_cs[:, h:h + 1] - a_cs_t[h:h + 1, :]
                decay = jnp.where(causal, jnp.exp(seg), 0.0)
                ms.append((cb * decay).astype(BF16))
            xp = xdt[:, pr * LANES:(pr + 1) * LANES].astype(BF16)
            zero = jnp.zeros_like(xp)
            x_bd = jnp.concatenate([jnp.where(lane_lt_half, xp, zero),
                                    jnp.where(lane_lt_half, zero, xp)], axis=0)
            y_pairs.append(_dot(jnp.concatenate(ms, axis=1), x_bd))
        y_diag = jnp.concatenate(y_pairs, axis=1)

        prev = state_ref[g]
        y_off = _dot(cm_b, prev.astype(BF16)) * ea_x[:, x0:x0 + GROUP_LANES]
        xw = (xdt * ds_x[:, x0:x0 + GROUP_LANES]).astype(BF16)
        new_state = _dot(bm.T.astype(BF16), xw)
        state_ref[g] = cd_x[:, x0:x0 + GROUP_LANES] * prev + new_state
        y_ref[pl.ds(r0, L), x0:x0 + GROUP_LANES] = y_diag + y_off


def _ssd_mix_kernel(h_ref, ya_ref, wz_ref, wxbc_ref, wdt_ref, wgb_ref, cw_ref, cb_ref,
                    dtb_ref, alog_ref, dskip_ref, nw_ref, wso_ref, wo_ref, lng_ref, lnb_ref,
                    o_ref, xraw, xbc, dtbuf, ybuf, state):
    tm = h_ref.shape[0]
    s = pl.program_id(1)
    h = h_ref[...]
    hb = h.astype(BF16)

    _load_halo(xraw, SSD_HALO, tm)
    xraw[SSD_HALO:SSD_HALO + tm, :] = _dot(hb, wxbc_ref[...])
    _causal_dwconv(xraw, cw_ref, cb_ref, xbc, SSD_HALO, tm, post=_silu)
    dtbuf[...] = jax.nn.softplus(_dot(hb, wdt_ref[...]) + dtb_ref[...])

    @pl.when(s == 0)
    def _():
        state[...] = jnp.zeros(state.shape, state.dtype)

    a_head = -jnp.exp(alog_ref[...])
    row = lax.broadcasted_iota(jnp.int32, (CHUNK, CHUNK), 0)
    col = lax.broadcasted_iota(jnp.int32, (CHUNK, CHUNK), 1)
    causal = row >= col
    tril = jnp.where(causal, 1.0, 0.0).astype(BF16)
    ek = lax.broadcasted_iota(jnp.int32, (LANES, SSM_DIM), 0)
    ej = lax.broadcasted_iota(jnp.int32, (LANES, SSM_DIM), 1)
    expand = jnp.where((ek < 3 * HEADS) & ((ek & (HEADS - 1)) == (ej >> 6)), 1.0, 0.0).astype(BF16)
    lane_lt_half = lax.broadcasted_iota(jnp.int32, (CHUNK, LANES), 1) < HEAD_DIM

    def chunk_body(c, carry):
        _ssd_chunk(c, xbc, dtbuf, ybuf, state, a_head, tril, causal, expand, lane_lt_half)
        return carry

    lax.fori_loop(0, tm // CHUNK, chunk_body, 0)

    z = _dot(hb, wz_ref[...])
    y = ybuf[...] + xbc[:, 0:SSM_DIM] * dskip_ref[...]
    yg = y * _silu(z)
    normed = []
    for g in range(GROUPS):
        blk = yg[:, g * GROUP_LANES:(g + 1) * GROUP_LANES]
        ms = jnp.mean(blk * blk, axis=-1, keepdims=True)
        normed.append(blk * lax.rsqrt(ms + RMS_EPS))
    yn = jnp.concatenate(normed, axis=1) * nw_ref[...]
    y_b = _dot(yn.astype(BF16), wso_ref[...])
    gate_b = jax.nn.sigmoid(_dot(hb, wgb_ref[...]))
    mix = _dot((ya_ref[...] + gate_b * y_b).astype(BF16), wo_ref[...])
    o_ref[...] = _layer_norm(DN_ALPHA * h + mix, lng_ref[...], lnb_ref[...])


def _ssd_mix(h, ya, wz, wxbc, wdt, wgb, cw, cb, dtb, alog, dskip, nw, wso, wo, lng, lnb, tm):
    bsz, t, d = h.shape
    weights = (wz, wxbc, wdt, wgb, cw, cb, dtb, alog, dskip, nw, wso, wo, lng, lnb)
    return pl.pallas_call(
        _ssd_mix_kernel,
        grid=(bsz, t // tm),
        in_specs=[_token_block(tm, d), _token_block(tm, d)] + [_resident(w.shape) for w in weights],
        out_specs=_token_block(tm, d),
        out_shape=jax.ShapeDtypeStruct((bsz, t, d), F32),
        scratch_shapes=[
            pltpu.VMEM((SSD_HALO + tm, XBC_DIM), F32),
            pltpu.VMEM((tm, XBC_DIM), F32),
            pltpu.VMEM((tm, LANES), F32),
            pltpu.VMEM((tm, SSM_DIM), F32),
            pltpu.VMEM((GROUPS, STATE, GROUP_LANES), F32),
        ],
        compiler_params=_seq_params(),
        name="ssd_mix",
    )(h, ya, *weights)


FFN_HALO = 8
FFN_COLS = CONV_COLS


def _ffn_kernel(h_ref, wup_ref, dww_ref, dwb_ref, wdown_ref, lng_ref, lnb_ref, o_ref,
                gbuf, vbuf, carry, act):
    tm = h_ref.shape[0]
    s = pl.program_id(1)
    h = h_ref[...]
    hb = h.astype(BF16)

    @pl.when(s == 0)
    def _():
        carry[...] = jnp.zeros(carry.shape, carry.dtype)

    def up_project(buf, c0):
        u = _dot(hb, wup_ref[:, c0:c0 + FFN_COLS])
        buf[0:FFN_HALO, :] = carry[:, c0:c0 + FFN_COLS]
        buf[FFN_HALO:FFN_HALO + tm, :] = u
        carry[:, c0:c0 + FFN_COLS] = u[tm - FFN_HALO:tm, :]

    for j in range(FFN_DIM // FFN_COLS):
        cg = j * FFN_COLS
        cv = FFN_DIM + cg
        up_project(gbuf, cg)
        up_project(vbuf, cv)

        def body(i, c, cg=cg, cv=cv):
            r0 = pl.multiple_of(i * CONV_ROWS, CONV_ROWS)
            gate = _dwconv_block(gbuf, dww_ref, dwb_ref, r0, FFN_HALO, 0, cg)
            val = _dwconv_block(vbuf, dww_ref, dwb_ref, r0, FFN_HALO, 0, cv)
            act[pl.ds(r0, CONV_ROWS), cg:cg + FFN_COLS] = (_silu(gate) * val).astype(BF16)
            return c

        lax.fori_loop(0, tm // CONV_ROWS, body, 0)

    ffn = _dot(act[...], wdown_ref[...])
    o_ref[...] = _layer_norm(DN_ALPHA * h + ffn, lng_ref[...], lnb_ref[...])


def _conv_ffn(h, wup, dww, dwb, wdown, lng, lnb, tm):
    bsz, t, d = h.shape
    weights = (wup, dww, dwb, wdown, lng, lnb)
    return pl.pallas_call(
        _ffn_kernel,
        grid=(bsz, t // tm),
        in_specs=[_token_block(tm, d)] + [_resident(w.shape) for w in weights],
        out_specs=_token_block(tm, d),
        out_shape=jax.ShapeDtypeStruct((bsz, t, d), F32),
        scratch_shapes=[
            pltpu.VMEM((FFN_HALO + tm, FFN_COLS), F32),
            pltpu.VMEM((FFN_HALO + tm, FFN_COLS), F32),
            pltpu.VMEM((FFN_HALO, 2 * FFN_DIM), F32),
            pltpu.VMEM((tm, FFN_DIM), BF16),
        ],
        compiler_params=_seq_params(),
        name="conv_ffn",
    )(h, *weights)


def _pick_tile(t, want):
    tm = min(want, t)
    assert t % tm == 0 and tm % CHUNK == 0, (t, tm)
    return tm


def kernel(x, ln_in_g, ln_in_b, w_in, conv_dw_w, conv_dw_b, conv_ln_g, conv_ln_b, w_conv_out, ssm_conv_w, ssm_conv_b, ssm_dt_bias, ssm_a_log, ssm_d, ssm_norm_w, w_ssm_out, w_o, ln1_g, ln1_b, w_ffn_up, ffn_dw_w, ffn_dw_b, w_ffn_down, ln2_g, ln2_b):
    bsz, t, d = x.shape
    assert d == D_MODEL
    tm_conf = _pick_tile(t, TILE_CONFORMER)
    tm_ssd = _pick_tile(t, TILE_SSD)
    tm_ffn = _pick_tile(t, TILE_FFN)
    row = lambda v: v.reshape(1, -1).astype(F32)
    rep = lambda v: jnp.tile(v.reshape(1, -1).astype(F32), (1, DT_REPLICAS))
    taps8 = lambda w: jnp.broadcast_to(w.astype(F32)[:, None, :], (w.shape[0], SUBLANES, w.shape[1]))

    o_glu, o_z = 0, 2 * CONV_DIM
    o_xbc = o_z + SSM_DIM
    o_dt = o_xbc + XBC_DIM
    o_ga = o_dt + HEADS
    o_gb = o_ga + D_MODEL

    h = _input_layer_norm(x, ln_in_g, ln_in_b, tm_ffn)
    for l in range(DEPTH):
        w = w_in[l].astype(BF16)
        ya = _conformer(
            h, w[:, o_glu:o_z], w[:, o_ga:o_gb], taps8(conv_dw_w[l]), row(conv_dw_b[l]),
            row(conv_ln_g[l]), row(conv_ln_b[l]), w_conv_out[l].astype(BF16), tm_conf)
        h = _ssd_mix(
            h, ya, w[:, o_z:o_xbc], w[:, o_xbc:o_dt],
            jnp.tile(w[:, o_dt:o_ga], (1, DT_REPLICAS)), w[:, o_gb:o_gb + D_MODEL],
            taps8(ssm_conv_w[l]), row(ssm_conv_b[l]), rep(ssm_dt_bias[l]), rep(ssm_a_log[l]),
            jnp.repeat(row(ssm_d[l]), HEAD_DIM, axis=1), row(ssm_norm_w[l]),
            w_ssm_out[l].astype(BF16), w_o[l].astype(BF16), row(ln1_g[l]), row(ln1_b[l]), tm_ssd)
        h = _conv_ffn(
            h, w_ffn_up[l].astype(BF16), taps8(ffn_dw_w[l]), row(ffn_dw_b[l]),
            w_ffn_down[l].astype(BF16), row(ln2_g[l]), row(ln2_b[l]), tm_ffn)
    return h
```

```python
import functools

import jax
import jax.numpy as jnp
from jax import lax
from jax.experimental import pallas as pl
from jax.experimental.pallas import tpu as pltpu

F32 = jnp.float32
BF16 = jnp.bfloat16

D_MODEL = 1024
DEPTH = 2
CONV_DIM = D_MODEL
CONV_KERNEL = 31
SSM_DIM = 2 * D_MODEL
HEAD_DIM = 64
HEADS = SSM_DIM // HEAD_DIM
GROUPS = 4
HEADS_PER_GROUP = HEADS // GROUPS
STATE = 128
SSM_CONV = 4
CHUNK = 128
GROUP_LANES = SSM_DIM // GROUPS
SSM_BC = 2 * GROUPS * STATE
XBC_DIM = SSM_DIM + SSM_BC
FFN_DIM = 2816
FFN_CONV = 3
DN_ALPHA = (2 * DEPTH) ** 0.25
LN_EPS = 1e-5
RMS_EPS = 1e-5

LANES = 128
SUBLANES = 8
VMEM_LIMIT_BYTES = 56 * 1024 * 1024

TILE_CONFORMER = 512
TILE_SSD = 256
TILE_FFN = 512


def _dot(a, b):
    return jnp.dot(a, b, preferred_element_type=F32)


def _layer_norm(x, g, b):
    mu = jnp.mean(x, axis=-1, keepdims=True)
    xc = x - mu
    var = jnp.mean(xc * xc, axis=-1, keepdims=True)
    return xc * lax.rsqrt(var + LN_EPS) * g + b


def _silu(x):
    return x * jax.nn.sigmoid(x)


def _resident(shape):
    zeros = (0,) * len(shape)
    return pl.BlockSpec(shape, lambda b, s: zeros, pipeline_mode=pl.Buffered(1))


def _token_block(tm, width):
    return pl.BlockSpec((None, tm, width), lambda b, s: (b, s, 0))


def _seq_params():
    return pltpu.CompilerParams(
        dimension_semantics=("arbitrary", "arbitrary"),
        vmem_limit_bytes=VMEM_LIMIT_BYTES)


def _load_halo(buf, halo, tm):
    s = pl.program_id(1)

    @pl.when(s == 0)
    def _():
        buf[0:halo, :] = jnp.zeros((halo, buf.shape[1]), buf.dtype)

    @pl.when(s > 0)
    def _():
        buf[0:halo, :] = buf[tm:tm + halo, :]


CONV_ROWS = 64
CONV_COLS = 256


def _dwconv_block(buf, w8_ref, b_ref, r0, halo, bc0, wc0):
    taps = w8_ref.shape[0]
    acc = None
    for r in range(min(SUBLANES, taps)):
        pad = SUBLANES if r else 0
        rows = CONV_ROWS + pad
        z = None
        for q in range((taps - 1 - r) // SUBLANES + 1):
            k = taps - 1 - (SUBLANES * q + r)
            start = r0 + (halo - SUBLANES * q - pad)
            if not isinstance(start, int):
                start = pl.multiple_of(start, SUBLANES)
            xw = buf[pl.ds(start, rows), bc0:bc0 + CONV_COLS]
            xw = xw.reshape(rows // SUBLANES, SUBLANES, CONV_COLS)
            term = (w8_ref[k, :, wc0:wc0 + CONV_COLS][None] * xw).reshape(rows, CONV_COLS)
            z = term if z is None else z + term
        zs = z if r == 0 else z[SUBLANES - r:SUBLANES - r + CONV_ROWS]
        acc = zs if acc is None else acc + zs
    return acc + b_ref[:, wc0:wc0 + CONV_COLS]


def _for_row_blocks(tm, body, unrolled):
    if unrolled:
        for r0 in range(0, tm, CONV_ROWS):
            body(r0)
    else:
        def step(i, carry):
            body(pl.multiple_of(i * CONV_ROWS, CONV_ROWS))
            return carry

        lax.fori_loop(0, tm // CONV_ROWS, step, 0)


def _causal_dwconv(buf, w8_ref, b_ref, out_ref, halo, tm, post=None, unrolled=False):
    for c0 in range(0, buf.shape[1], CONV_COLS):
        def body(r0, c0=c0):
            blk = _dwconv_block(buf, w8_ref, b_ref, r0, halo, c0, c0)
            out_ref[pl.ds(r0, CONV_ROWS), c0:c0 + CONV_COLS] = blk if post is None else post(blk)

        _for_row_blocks(tm, body, unrolled)


def _ln_kernel(x_ref, g_ref, b_ref, o_ref):
    o_ref[...] = _layer_norm(x_ref[...], g_ref[...], b_ref[...])


def _input_layer_norm(x, g, b, tm):
    bsz, t, d = x.shape
    return pl.pallas_call(
        _ln_kernel,
        grid=(bsz, t // tm),
        in_specs=[_token_block(tm, d), _resident((1, d)), _resident((1, d))],
        out_specs=_token_block(tm, d),
        out_shape=jax.ShapeDtypeStruct(x.shape, F32),
        compiler_params=_seq_params(),
        name="input_ln",
    )(x, g.reshape(1, d), b.reshape(1, d))


CONF_HALO = 32


def _conformer_kernel(h_ref, wglu_ref, wga_ref, dww_ref, dwb_ref, lng_ref, lnb_ref, wout_ref,
                      o_ref, vbuf, cbuf):
    tm = h_ref.shape[0]
    hb = h_ref[...].astype(BF16)
    a = _dot(hb, wglu_ref[:, :CONV_DIM])
    g = _dot(hb, wglu_ref[:, CONV_DIM:])
    _load_halo(vbuf, CONF_HALO, tm)
    vbuf[CONF_HALO:CONF_HALO + tm, :] = a * jax.nn.sigmoid(g)
    _causal_dwconv(vbuf, dww_ref, dwb_ref, cbuf, CONF_HALO, tm, unrolled=True)
    v = _silu(_layer_norm(cbuf[...], lng_ref[...], lnb_ref[...]))
    y = _dot(v.astype(BF16), wout_ref[...])
    gate = jax.nn.sigmoid(_dot(hb, wga_ref[...]))
    o_ref[...] = gate * y


def _conformer(h, wglu, wga, dww, dwb, lng, lnb, wout, tm):
    bsz, t, d = h.shape
    return pl.pallas_call(
        _conformer_kernel,
        grid=(bsz, t // tm),
        in_specs=[
            _token_block(tm, d),
            _resident(wglu.shape), _resident(wga.shape), _resident(dww.shape),
            _resident(dwb.shape), _resident(lng.shape), _resident(lnb.shape),
            _resident(wout.shape),
        ],
        out_specs=_token_block(tm, d),
        out_shape=jax.ShapeDtypeStruct((bsz, t, d), F32),
        scratch_shapes=[
            pltpu.VMEM((CONF_HALO + tm, CONV_DIM), F32),
            pltpu.VMEM((tm, CONV_DIM), F32),
        ],
        compiler_params=_seq_params(),
        name="conformer",
    )(h, wglu, wga, dww, dwb, lng, lnb, wout)


SSD_HALO = 8
DT_REPLICAS = LANES // HEADS


def _split3(x):
    hi = x.astype(BF16)
    r1 = x - hi.astype(F32)
    mid = r1.astype(BF16)
    lo = (r1 - mid.astype(F32)).astype(BF16)
    return hi, mid, lo


def _ssd_chunk(c, xbc_ref, dt_ref, y_ref, state_ref, a_head, tril, causal, expand, lane_lt_half):
    L = CHUNK
    r0 = c * L
    dt = dt_ref[pl.ds(r0, L), :]
    a = dt * a_head
    hi, mid, lo = _split3(a)
    p = _dot(tril, jnp.concatenate([hi, mid, lo], axis=1))
    a_cs = p[:, 0:LANES] + p[:, LANES:2 * LANES] + p[:, 2 * LANES:3 * LANES]
    a_cs_t = a_cs.T
    a_last = a_cs[L - 1:L, :]
    decay_states = jnp.exp(a_last - a_cs)
    exp_a = jnp.exp(a_cs)

    lane = lax.broadcasted_iota(jnp.int32, (3 * L, LANES), 1)
    s_hi, s_mid, s_lo = _split3(jnp.concatenate([dt, decay_states, exp_a], axis=0))
    pieces = jnp.where(lane < HEADS, s_hi, jnp.where(lane < 2 * HEADS, s_mid, s_lo))
    ex = _dot(pieces, expand)
    dt_x = ex[0:L]
    ds_x = ex[L:2 * L]
    ea_x = ex[2 * L:3 * L]
    cd_x = ea_x[L - 1:L, :]

    for g in range(GROUPS):
        x0 = g * GROUP_LANES
        xs = xbc_ref[pl.ds(r0, L), x0:x0 + GROUP_LANES]
        bm = xbc_ref[pl.ds(r0, L), SSM_DIM + g * STATE:SSM_DIM + (g + 1) * STATE]
        cm = xbc_ref[pl.ds(r0, L),
                     SSM_DIM + GROUPS * STATE + g * STATE:SSM_DIM + GROUPS * STATE + (g + 1) * STATE]
        bm_b = bm.astype(BF16)
        cm_b = cm.astype(BF16)
        xdt = xs * dt_x[:, x0:x0 + GROUP_LANES]
        cb = lax.dot_general(cm_b, bm_b, (((1,), (1,)), ((), ())), preferred_element_type=F32)

        y_pairs = []
        for pr in range(HEADS_PER_GROUP // 2):
            ms = []
            for hh in range(2):
                h = g * HEADS_PER_GROUP + 2 * pr + hh
                seg = a_cs[:, h:h + 1] - a_cs_t[h:h + 1, :]
                decay = jnp.where(causal, jnp.exp(seg), 0.0)
                ms.append((cb * decay).astype(BF16))
            xp = xdt[:, pr * LANES:(pr + 1) * LANES].astype(BF16)
            zero = jnp.zeros_like(xp)
            x_bd = jnp.concatenate([jnp.where(lane_lt_half, xp, zero),
                                    jnp.where(lane_lt_half, zero, xp)], axis=0)
            y_pairs.append(_dot(jnp.concatenate(ms, axis=1), x_bd))
        y_diag = jnp.concatenate(y_pairs, axis=1)

        prev = state_ref[g]
        y_off = _dot(cm_b, prev.astype(BF16)) * ea_x[:, x0:x0 + GROUP_LANES]
        xw = (xdt * ds_x[:, x0:x0 + GROUP_LANES]).astype(BF16)
        new_state = _dot(bm.T.astype(BF16), xw)
        state_ref[g] = cd_x[:, x0:x0 + GROUP_LANES] * prev + new_state
        y_ref[pl.ds(r0, L), x0:x0 + GROUP_LANES] = y_diag + y_off


def _ssd_mix_kernel(h_ref, ya_ref, wz_ref, wxbc_ref, wdt_ref, wgb_ref, cw_ref, cb_ref,
                    dtb_ref, alog_ref, dskip_ref, nw_ref, wso_ref, wo_ref, lng_ref, lnb_ref,
                    o_ref, xraw, xbc, dtbuf, ybuf, state):
    tm = h_ref.shape[0]
    s = pl.program_id(1)
    h = h_ref[...]
    hb = h.astype(BF16)

    _load_halo(xraw, SSD_HALO, tm)
    xraw[SSD_HALO:SSD_HALO + tm, :] = _dot(hb, wxbc_ref[...])
    _causal_dwconv(xraw, cw_ref, cb_ref, xbc, SSD_HALO, tm, post=_silu, unrolled=True)
    dtbuf[...] = jax.nn.softplus(_dot(hb, wdt_ref[...]) + dtb_ref[...])

    @pl.when(s == 0)
    def _():
        state[...] = jnp.zeros(state.shape, state.dtype)

    a_head = -jnp.exp(alog_ref[...])
    row = lax.broadcasted_iota(jnp.int32, (CHUNK, CHUNK), 0)
    col = lax.broadcasted_iota(jnp.int32, (CHUNK, CHUNK), 1)
    causal = row >= col
    tril = jnp.where(causal, 1.0, 0.0).astype(BF16)
    ek = lax.broadcasted_iota(jnp.int32, (LANES, SSM_DIM), 0)
    ej = lax.broadcasted_iota(jnp.int32, (LANES, SSM_DIM), 1)
    expand = jnp.where((ek < 3 * HEADS) & ((ek & (HEADS - 1)) == (ej >> 6)), 1.0, 0.0).astype(BF16)
    lane_lt_half = lax.broadcasted_iota(jnp.int32, (CHUNK, LANES), 1) < HEAD_DIM

    for c in range(tm // CHUNK):
        _ssd_chunk(c, xbc, dtbuf, ybuf, state, a_head, tril, causal, expand, lane_lt_half)

    z = _dot(hb, wz_ref[...])
    y = ybuf[...] + xbc[:, 0:SSM_DIM] * dskip_ref[...]
    yg = y * _silu(z)
    normed = []
    for g in range(GROUPS):
        blk = yg[:, g * GROUP_LANES:(g + 1) * GROUP_LANES]
        ms = jnp.mean(blk * blk, axis=-1, keepdims=True)
        normed.append(blk * lax.rsqrt(ms + RMS_EPS))
    yn = jnp.concatenate(normed, axis=1) * nw_ref[...]
    y_b = _dot(yn.astype(BF16), wso_ref[...])
    gate_b = jax.nn.sigmoid(_dot(hb, wgb_ref[...]))
    mix = _dot((ya_ref[...] + gate_b * y_b).astype(BF16), wo_ref[...])
    o_ref[...] = _layer_norm(DN_ALPHA * h + mix, lng_ref[...], lnb_ref[...])


def _ssd_mix(h, ya, wz, wxbc, wdt, wgb, cw, cb, dtb, alog, dskip, nw, wso, wo, lng, lnb, tm):
    bsz, t, d = h.shape
    weights = (wz, wxbc, wdt, wgb, cw, cb, dtb, alog, dskip, nw, wso, wo, lng, lnb)
    return pl.pallas_call(
        _ssd_mix_kernel,
        grid=(bsz, t // tm),
        in_specs=[_token_block(tm, d), _token_block(tm, d)] + [_resident(w.shape) for w in weights],
        out_specs=_token_block(tm, d),
        out_shape=jax.ShapeDtypeStruct((bsz, t, d), F32),
        scratch_shapes=[
            pltpu.VMEM((SSD_HALO + tm, XBC_DIM), F32),
            pltpu.VMEM((tm, XBC_DIM), F32),
            pltpu.VMEM((tm, LANES), F32),
            pltpu.VMEM((tm, SSM_DIM), F32),
            pltpu.VMEM((GROUPS, STATE, GROUP_LANES), F32),
        ],
        compiler_params=_seq_params(),
        name="ssd_mix",
    )(h, ya, *weights)


FFN_HALO = 8
FFN_COLS = CONV_COLS


def _ffn_kernel(h_ref, wup_ref, dww_ref, dwb_ref, wdown_ref, lng_ref, lnb_ref, o_ref,
                gbuf, vbuf, carry, act):
    tm = h_ref.shape[0]
    s = pl.program_id(1)
    h = h_ref[...]
    hb = h.astype(BF16)

    @pl.when(s == 0)
    def _():
        carry[...] = jnp.zeros(carry.shape, carry.dtype)

    def up_project(buf, c0):
        u = _dot(hb, wup_ref[:, c0:c0 + FFN_COLS])
        buf[0:FFN_HALO, :] = carry[:, c0:c0 + FFN_COLS]
        buf[FFN_HALO:FFN_HALO + tm, :] = u
        carry[:, c0:c0 + FFN_COLS] = u[tm - FFN_HALO:tm, :]

    for j in range(FFN_DIM // FFN_COLS):
        cg = j * FFN_COLS
        cv = FFN_DIM + cg
        gb = gbuf.at[j % 2]
        vb = vbuf.at[j % 2]
        up_project(gb, cg)
        up_project(vb, cv)

        def body(r0, cg=cg, cv=cv, gb=gb, vb=vb):
            gate = _dwconv_block(gb, dww_ref, dwb_ref, r0, FFN_HALO, 0, cg)
            val = _dwconv_block(vb, dww_ref, dwb_ref, r0, FFN_HALO, 0, cv)
            act[pl.ds(r0, CONV_ROWS), cg:cg + FFN_COLS] = (_silu(gate) * val).astype(BF16)

        _for_row_blocks(tm, body, unrolled=True)

    ffn = _dot(act[...], wdown_ref[...])
    o_ref[...] = _layer_norm(DN_ALPHA * h + ffn, lng_ref[...], lnb_ref[...])


def _conv_ffn(h, wup, dww, dwb, wdown, lng, lnb, tm):
    bsz, t, d = h.shape
    weights = (wup, dww, dwb, wdown, lng, lnb)
    return pl.pallas_call(
        _ffn_kernel,
        grid=(bsz, t // tm),
        in_specs=[_token_block(tm, d)] + [_resident(w.shape) for w in weights],
        out_specs=_token_block(tm, d),
        out_shape=jax.ShapeDtypeStruct((bsz, t, d), F32),
        scratch_shapes=[
            pltpu.VMEM((2, FFN_HALO + tm, FFN_COLS), F32),
            pltpu.VMEM((2, FFN_HALO + tm, FFN_COLS), F32),
            pltpu.VMEM((FFN_HALO, 2 * FFN_DIM), F32),
            pltpu.VMEM((tm, FFN_DIM), BF16),
        ],
        compiler_params=_seq_params(),
        name="conv_ffn",
    )(h, *weights)


def _pick_tile(t, want):
    tm = min(want, t)
    assert t % tm == 0 and tm % CHUNK == 0, (t, tm)
    return tm


def kernel(x, ln_in_g, ln_in_b, w_in, conv_dw_w, conv_dw_b, conv_ln_g, conv_ln_b, w_conv_out, ssm_conv_w, ssm_conv_b, ssm_dt_bias, ssm_a_log, ssm_d, ssm_norm_w, w_ssm_out, w_o, ln1_g, ln1_b, w_ffn_up, ffn_dw_w, ffn_dw_b, w_ffn_down, ln2_g, ln2_b):
    bsz, t, d = x.shape
    assert d == D_MODEL
    tm_conf = _pick_tile(t, TILE_CONFORMER)
    tm_ssd = _pick_tile(t, TILE_SSD)
    tm_ffn = _pick_tile(t, TILE_FFN)
    row = lambda v: v.reshape(1, -1).astype(F32)
    rep = lambda v: jnp.tile(v.reshape(1, -1).astype(F32), (1, DT_REPLICAS))
    taps8 = lambda w: jnp.broadcast_to(w.astype(F32)[:, None, :], (w.shape[0], SUBLANES, w.shape[1]))

    o_glu, o_z = 0, 2 * CONV_DIM
    o_xbc = o_z + SSM_DIM
    o_dt = o_xbc + XBC_DIM
    o_ga = o_dt + HEADS
    o_gb = o_ga + D_MODEL

    h = _input_layer_norm(x, ln_in_g, ln_in_b, tm_ffn)
    for l in range(DEPTH):
        w = w_in[l].astype(BF16)
        ya = _conformer(
            h, w[:, o_glu:o_z], w[:, o_ga:o_gb], taps8(conv_dw_w[l]), row(conv_dw_b[l]),
            row(conv_ln_g[l]), row(conv_ln_b[l]), w_conv_out[l].astype(BF16), tm_conf)
        h = _ssd_mix(
            h, ya, w[:, o_z:o_xbc], w[:, o_xbc:o_dt],
            jnp.tile(w[:, o_dt:o_ga], (1, DT_REPLICAS)), w[:, o_gb:o_gb + D_MODEL],
            taps8(ssm_conv_w[l]), row(ssm_conv_b[l]), rep(ssm_dt_bias[l]), rep(ssm_a_log[l]),
            jnp.repeat(row(ssm_d[l]), HEAD_DIM, axis=1), row(ssm_norm_w[l]),
            w_ssm_out[l].astype(BF16), w_o[l].astype(BF16), row(ln1_g[l]), row(ln1_b[l]), tm_ssd)
        h = _conv_ffn(
            h, w_ffn_up[l].astype(BF16), taps8(ffn_dw_w[l]), row(ffn_dw_b[l]),
            w_ffn_down[l].astype(BF16), row(ln2_g[l]), row(ln2_b[l]), tm_ffn)
    return h
```

```python
import functools

import jax
import jax.numpy as jnp
from jax import lax
from jax.experimental import pallas as pl
from jax.experimental.pallas import tpu as pltpu

F32 = jnp.float32
BF16 = jnp.bfloat16

D_MODEL = 1024
DEPTH = 2
CONV_DIM = D_MODEL
CONV_KERNEL = 31
SSM_DIM = 2 * D_MODEL
HEAD_DIM = 64
HEADS = SSM_DIM // HEAD_DIM
GROUPS = 4
HEADS_PER_GROUP = HEADS // GROUPS
STATE = 128
SSM_CONV = 4
CHUNK = 128
GROUP_LANES = SSM_DIM // GROUPS
SSM_BC = 2 * GROUPS * STATE
XBC_DIM = SSM_DIM + SSM_BC
FFN_DIM = 2816
FFN_CONV = 3
DN_ALPHA = (2 * DEPTH) ** 0.25
LN_EPS = 1e-5
RMS_EPS = 1e-5

LANES = 128
SUBLANES = 8
MXU_COLS = 256
VMEM_LIMIT_BYTES = 56 * 1024 * 1024

TILE_MIXER = 256
TILE_FFN = 512

CONV_ROWS = 128
CONV_COLS = 128
CONF_HALO = 32
SSD_HALO = 8
FFN_HALO = 8
DT_REPLICAS = LANES // HEADS


def _dot(a, b):
    return jnp.dot(a, b, preferred_element_type=F32)


def _layer_norm(x, g, b):
    mu = jnp.mean(x, axis=-1, keepdims=True)
    xc = x - mu
    var = jnp.mean(xc * xc, axis=-1, keepdims=True)
    return xc * lax.rsqrt(var + LN_EPS) * g + b


def _silu(x):
    return x * jax.nn.sigmoid(x)


def _resident(shape):
    zeros = (0,) * len(shape)
    return pl.BlockSpec(shape, lambda b, s: zeros, pipeline_mode=pl.Buffered(1))


def _token_block(tm, width):
    return pl.BlockSpec((None, tm, width), lambda b, s: (b, s, 0))


def _seq_params():
    return pltpu.CompilerParams(
        dimension_semantics=("arbitrary", "arbitrary"),
        vmem_limit_bytes=VMEM_LIMIT_BYTES)


def _dwconv_block(buf, w8_ref, b_ref, r0, halo, bc0, wc0):
    taps = w8_ref.shape[0]
    acc = None
    for r in range(min(SUBLANES, taps)):
        pad = SUBLANES if r else 0
        rows = CONV_ROWS + pad
        z = None
        for q in range((taps - 1 - r) // SUBLANES + 1):
            k = taps - 1 - (SUBLANES * q + r)
            start = r0 + halo - SUBLANES * q - pad
            xw = buf[start:start + rows, bc0:bc0 + CONV_COLS]
            xw = xw.reshape(rows // SUBLANES, SUBLANES, CONV_COLS)
            term = (w8_ref[k, :, wc0:wc0 + CONV_COLS][None] * xw).reshape(rows, CONV_COLS)
            z = term if z is None else z + term
        zs = z if r == 0 else z[SUBLANES - r:SUBLANES - r + CONV_ROWS]
        acc = zs if acc is None else acc + zs
    return acc + b_ref[:, wc0:wc0 + CONV_COLS]


def _dwconv_cols(buf, w8_ref, b_ref, out_ref, halo, tm, c0, width, post=None):
    for c in range(c0, c0 + width, CONV_COLS):
        for r0 in range(0, tm, CONV_ROWS):
            blk = _dwconv_block(buf, w8_ref, b_ref, r0, halo, c, c)
            out_ref[r0:r0 + CONV_ROWS, c:c + CONV_COLS] = blk if post is None else post(blk)


def _roll_halo(buf, halo, tm, first):
    @pl.when(first)
    def _():
        buf[0:halo, :] = jnp.zeros((halo, buf.shape[1]), buf.dtype)

    @pl.when(jnp.logical_not(first))
    def _():
        buf[0:halo, :] = buf[tm:tm + halo, :]


def _split3(x):
    hi = x.astype(BF16)
    r1 = x - hi.astype(F32)
    mid = r1.astype(BF16)
    lo = (r1 - mid.astype(F32)).astype(BF16)
    return hi, mid, lo


def _ssd_chunk(c, xbc_ref, dt_ref, y_ref, state_ref, a_head, tril, causal, expand, lane_lt_half):
    L = CHUNK
    r0 = c * L
    dt = dt_ref[r0:r0 + L, :]
    a = dt * a_head
    hi, mid, lo = _split3(a)
    p = _dot(tril, jnp.concatenate([hi, mid, lo], axis=1))
    a_cs = p[:, 0:LANES] + p[:, LANES:2 * LANES] + p[:, 2 * LANES:3 * LANES]
    a_cs_t = a_cs.T
    a_last = a_cs[L - 1:L, :]
    decay_states = jnp.exp(a_last - a_cs)
    exp_a = jnp.exp(a_cs)

    lane = lax.broadcasted_iota(jnp.int32, (3 * L, LANES), 1)
    s_hi, s_mid, s_lo = _split3(jnp.concatenate([dt, decay_states, exp_a], axis=0))
    pieces = jnp.where(lane < HEADS, s_hi, jnp.where(lane < 2 * HEADS, s_mid, s_lo))
    ex = _dot(pieces, expand)
    dt_x = ex[0:L]
    ds_x = ex[L:2 * L]
    ea_x = ex[2 * L:3 * L]
    cd_x = ea_x[L - 1:L, :]

    for g in range(GROUPS):
        x0 = g * GROUP_LANES
        b0 = SSM_DIM + g * STATE
        c0 = SSM_DIM + GROUPS * STATE + g * STATE
        xs = xbc_ref[r0:r0 + L, x0:x0 + GROUP_LANES]
        bm = xbc_ref[r0:r0 + L, b0:b0 + STATE]
        cm = xbc_ref[r0:r0 + L, c0:c0 + STATE]
        bm_b = bm.astype(BF16)
        cm_b = cm.astype(BF16)
        xdt = xs * dt_x[:, x0:x0 + GROUP_LANES]
        cb = lax.dot_general(cm_b, bm_b, (((1,), (1,)), ((), ())), preferred_element_type=F32)

        y_pairs = []
        for pr in range(HEADS_PER_GROUP // 2):
            ms = []
            for hh in range(2):
                h = g * HEADS_PER_GROUP + 2 * pr + hh
                seg = a_cs[:, h:h + 1] - a_cs_t[h:h + 1, :]
                decay = jnp.where(causal, jnp.exp(seg), 0.0)
                ms.append((cb * decay).astype(BF16))
            xp = xdt[:, pr * LANES:(pr + 1) * LANES].astype(BF16)
            zero = jnp.zeros_like(xp)
            x_bd = jnp.concatenate([jnp.where(lane_lt_half, xp, zero),
                                    jnp.where(lane_lt_half, zero, xp)], axis=0)
            y_pairs.append(_dot(jnp.concatenate(ms, axis=1), x_bd))
        y_diag = jnp.concatenate(y_pairs, axis=1)

        prev = state_ref[g]
        y_off = _dot(cm_b, prev.astype(BF16)) * ea_x[:, x0:x0 + GROUP_LANES]
        xw = (xdt * ds_x[:, x0:x0 + GROUP_LANES]).astype(BF16)
        new_state = _dot(bm.T.astype(BF16), xw)
        state_ref[g] = cd_x[:, x0:x0 + GROUP_LANES] * prev + new_state
        y_ref[r0:r0 + L, x0:x0 + GROUP_LANES] = y_diag + y_off


def _mixer_kernel(h_ref, ln0g_ref, ln0b_ref,
                  wglu_ref, wga_ref, cdw_ref, cdb_ref, clng_ref, clnb_ref, wco_ref,
                  wz_ref, wxbc_ref, wdt_ref, wgb_ref, scw_ref, scb_ref, dtb_ref, alog_ref,
                  dskip_ref, nw_ref, wso_ref, wo_ref, lng_ref, lnb_ref,
                  o_ref,
                  vbuf, cbuf, xraw, xbc, dtbuf, ybuf, zbuf, state, *, input_ln):
    tm = h_ref.shape[0]
    first = pl.program_id(1) == 0
    _roll_halo(vbuf, CONF_HALO, tm, first)
    _roll_halo(xraw, SSD_HALO, tm, first)

    @pl.when(first)
    def _():
        state[...] = jnp.zeros(state.shape, state.dtype)

    h = h_ref[...]
    if input_ln:
        h = _layer_norm(h, ln0g_ref[...], ln0b_ref[...])
    hb = h.astype(BF16)
    dtbuf[...] = jax.nn.softplus(_dot(hb, wdt_ref[...]) + dtb_ref[...])

    def glu_block(c0):
        a = _dot(hb, wglu_ref[:, c0:c0 + MXU_COLS])
        g = _dot(hb, wglu_ref[:, CONV_DIM + c0:CONV_DIM + c0 + MXU_COLS])
        vbuf[CONF_HALO:CONF_HALO + tm, c0:c0 + MXU_COLS] = a * jax.nn.sigmoid(g)
        _dwconv_cols(vbuf, cdw_ref, cdb_ref, cbuf, CONF_HALO, tm, c0, MXU_COLS)

    def xbc_block(c0):
        xraw[SSD_HALO:SSD_HALO + tm, c0:c0 + MXU_COLS] = _dot(hb, wxbc_ref[:, c0:c0 + MXU_COLS])
        _dwconv_cols(xraw, scw_ref, scb_ref, xbc, SSD_HALO, tm, c0, MXU_COLS, post=_silu)

    def z_block(c0):
        zbuf[:, c0:c0 + MXU_COLS] = _dot(hb, wz_ref[:, c0:c0 + MXU_COLS])

    n_glu = CONV_DIM // MXU_COLS
    n_xbc = XBC_DIM // MXU_COLS
    n_z = SSM_DIM // MXU_COLS
    for i in range(n_glu):
        glu_block(i * MXU_COLS)
        for j in range(i * n_xbc // n_glu, (i + 1) * n_xbc // n_glu):
            xbc_block(j * MXU_COLS)
        for j in range(i * n_z // n_glu, (i + 1) * n_z // n_glu):
            z_block(j * MXU_COLS)

    a_head = -jnp.exp(alog_ref[...])
    row = lax.broadcasted_iota(jnp.int32, (CHUNK, CHUNK), 0)
    col = lax.broadcasted_iota(jnp.int32, (CHUNK, CHUNK), 1)
    causal = row >= col
    tril = jnp.where(causal, 1.0, 0.0).astype(BF16)
    ek = lax.broadcasted_iota(jnp.int32, (LANES, SSM_DIM), 0)
    ej = lax.broadcasted_iota(jnp.int32, (LANES, SSM_DIM), 1)
    expand = jnp.where((ek < 3 * HEADS) & ((ek & (HEADS - 1)) == (ej >> 6)), 1.0, 0.0).astype(BF16)
    lane_lt_half = lax.broadcasted_iota(jnp.int32, (CHUNK, LANES), 1) < HEAD_DIM

    y_a = None
    for c in range(tm // CHUNK):
        _ssd_chunk(c, xbc, dtbuf, ybuf, state, a_head, tril, causal, expand, lane_lt_half)
        if c == 0:
            v = _silu(_layer_norm(cbuf[...], clng_ref[...], clnb_ref[...]))
            gate_a = jax.nn.sigmoid(_dot(hb, wga_ref[...]))
            y_a = gate_a * _dot(v.astype(BF16), wco_ref[...])
    gate_b = jax.nn.sigmoid(_dot(hb, wgb_ref[...]))

    y = ybuf[...] + xbc[:, 0:SSM_DIM] * dskip_ref[...]
    yg = y * _silu(zbuf[...])
    normed = []
    for g in range(GROUPS):
        blk = yg[:, g * GROUP_LANES:(g + 1) * GROUP_LANES]
        ms = jnp.mean(blk * blk, axis=-1, keepdims=True)
        normed.append(blk * lax.rsqrt(ms + RMS_EPS))
    yn = jnp.concatenate(normed, axis=1) * nw_ref[...]
    y_b = _dot(yn.astype(BF16), wso_ref[...])
    mix = _dot((y_a + gate_b * y_b).astype(BF16), wo_ref[...])
    o_ref[...] = _layer_norm(DN_ALPHA * h + mix, lng_ref[...], lnb_ref[...])


def _mixer(h, weights, tm, input_ln):
    bsz, t, d = h.shape
    return pl.pallas_call(
        functools.partial(_mixer_kernel, input_ln=input_ln),
        grid=(bsz, t // tm),
        in_specs=[_token_block(tm, d)] + [_resident(w.shape) for w in weights],
        out_specs=_token_block(tm, d),
        out_shape=jax.ShapeDtypeStruct((bsz, t, d), F32),
        scratch_shapes=[
            pltpu.VMEM((CONF_HALO + tm, CONV_DIM), F32),
            pltpu.VMEM((tm, CONV_DIM), F32),
            pltpu.VMEM((SSD_HALO + tm, XBC_DIM), F32),
            pltpu.VMEM((tm, XBC_DIM), F32),
            pltpu.VMEM((tm, LANES), F32),
            pltpu.VMEM((tm, SSM_DIM), F32),
            pltpu.VMEM((tm, SSM_DIM), F32),
            pltpu.VMEM((GROUPS, STATE, GROUP_LANES), F32),
        ],
        compiler_params=_seq_params(),
        name="mixer",
    )(h, *weights)


def _ffn_kernel(h_ref, wup_ref, dww_ref, dwb_ref, wdown_ref, lng_ref, lnb_ref, o_ref,
                gbuf, vbuf, carry, act):
    tm = h_ref.shape[0]
    h = h_ref[...]
    hb = h.astype(BF16)

    @pl.when(pl.program_id(1) == 0)
    def _():
        carry[...] = jnp.zeros(carry.shape, carry.dtype)

    def up_project(buf, c0):
        u = _dot(hb, wup_ref[:, c0:c0 + MXU_COLS])
        buf[0:FFN_HALO, :] = carry[:, c0:c0 + MXU_COLS]
        buf[FFN_HALO:FFN_HALO + tm, :] = u
        carry[:, c0:c0 + MXU_COLS] = u[tm - FFN_HALO:tm, :]

    for j in range(FFN_DIM // MXU_COLS):
        cg = j * MXU_COLS
        cv = FFN_DIM + cg
        gb = gbuf.at[j % 2]
        vb = vbuf.at[j % 2]
        up_project(gb, cg)
        up_project(vb, cv)
        for c in range(0, MXU_COLS, CONV_COLS):
            for r0 in range(0, tm, CONV_ROWS):
                gate = _dwconv_block(gb, dww_ref, dwb_ref, r0, FFN_HALO, c, cg + c)
                val = _dwconv_block(vb, dww_ref, dwb_ref, r0, FFN_HALO, c, cv + c)
                act[r0:r0 + CONV_ROWS, cg + c:cg + c + CONV_COLS] = (
                    _silu(gate) * val).astype(BF16)

    ffn = _dot(act[...], wdown_ref[...])
    o_ref[...] = _layer_norm(DN_ALPHA * h + ffn, lng_ref[...], lnb_ref[...])


def _conv_ffn(h, wup, dww, dwb, wdown, lng, lnb, tm):
    bsz, t, d = h.shape
    weights = (wup, dww, dwb, wdown, lng, lnb)
    return pl.pallas_call(
        _ffn_kernel,
        grid=(bsz, t // tm),
        in_specs=[_token_block(tm, d)] + [_resident(w.shape) for w in weights],
        out_specs=_token_block(tm, d),
        out_shape=jax.ShapeDtypeStruct((bsz, t, d), F32),
        scratch_shapes=[
            pltpu.VMEM((2, FFN_HALO + tm, MXU_COLS), F32),
            pltpu.VMEM((2, FFN_HALO + tm, MXU_COLS), F32),
            pltpu.VMEM((FFN_HALO, 2 * FFN_DIM), F32),
            pltpu.VMEM((tm, FFN_DIM), BF16),
        ],
        compiler_params=_seq_params(),
        name="conv_ffn",
    )(h, *weights)


def _pick_tile(t, want):
    tm = min(want, t)
    assert t % tm == 0 and tm % CHUNK == 0, (t, tm)
    return tm


def kernel(x, ln_in_g, ln_in_b, w_in, conv_dw_w, conv_dw_b, conv_ln_g, conv_ln_b, w_conv_out, ssm_conv_w, ssm_conv_b, ssm_dt_bias, ssm_a_log, ssm_d, ssm_norm_w, w_ssm_out, w_o, ln1_g, ln1_b, w_ffn_up, ffn_dw_w, ffn_dw_b, w_ffn_down, ln2_g, ln2_b):
    bsz, t, d = x.shape
    assert d == D_MODEL
    tm_mix = _pick_tile(t, TILE_MIXER)
    tm_ffn = _pick_tile(t, TILE_FFN)
    row = lambda v: v.reshape(1, -1).astype(F32)
    rep = lambda v: jnp.tile(v.reshape(1, -1).astype(F32), (1, DT_REPLICAS))
    taps8 = lambda w: jnp.broadcast_to(w.astype(F32)[:, None, :], (w.shape[0], SUBLANES, w.shape[1]))

    o_glu, o_z = 0, 2 * CONV_DIM
    o_xbc = o_z + SSM_DIM
    o_dt = o_xbc + XBC_DIM
    o_ga = o_dt + HEADS
    o_gb = o_ga + D_MODEL

    h = x
    for l in range(DEPTH):
        w = w_in[l].astype(BF16)
        mixer_weights = (
            row(ln_in_g), row(ln_in_b),
            w[:, o_glu:o_z], w[:, o_ga:o_gb], taps8(conv_dw_w[l]), row(conv_dw_b[l]),
            row(conv_ln_g[l]), row(conv_ln_b[l]), w_conv_out[l].astype(BF16),
            w[:, o_z:o_xbc], w[:, o_xbc:o_dt], jnp.tile(w[:, o_dt:o_ga], (1, DT_REPLICAS)),
            w[:, o_gb:o_gb + D_MODEL], taps8(ssm_conv_w[l]), row(ssm_conv_b[l]),
            rep(ssm_dt_bias[l]), rep(ssm_a_log[l]), jnp.repeat(row(ssm_d[l]), HEAD_DIM, axis=1),
            row(ssm_norm_w[l]), w_ssm_out[l].astype(BF16), w_o[l].astype(BF16),
            row(ln1_g[l]), row(ln1_b[l]))
        h = _mixer(h, mixer_weights, tm_mix, input_ln=(l == 0))
        h = _conv_ffn(
            h, w_ffn_up[l].astype(BF16), taps8(ffn_dw_w[l]), row(ffn_dw_b[l]),
            w_ffn_down[l].astype(BF16), row(ln2_g[l]), row(ln2_b[l]), tm_ffn)
    return h
```
